```python
import jax, jax.numpy as jnp
from jax import lax
import numpy as np

D_MODEL = 1024
BATCH = 4
SEQ = 8192
DEPTH = 1
DEC_BATCH = 16
DEC_SEQ = 4096
PAST_LEN = 128

D_MIX = D_MODEL
D_FOURIER = D_MIX // 2
N_F_HEADS = 4
F_HEAD_DIM = D_FOURIER // N_F_HEADS
D_CONV = D_MIX - D_FOURIER
N_C_GROUPS = 4
CONV_WIDTH = 3
D_IN = D_FOURIER + 3 * D_CONV
D_FF = 2816
RMS_EPS = 1e-6
FFN_RES_SCALE = 0.5

kernel_name = "hybrid_fnet_shortconv_macaron_encoder"


def rms_norm(x, g):
    xf = x.astype(jnp.float32)
    y = xf * lax.rsqrt(jnp.mean(xf * xf, axis=-1, keepdims=True) + RMS_EPS)
    return (y * g.astype(jnp.float32)).astype(x.dtype)


def swiglu(h, w_gate, w_up, w_down):
    return (jax.nn.silu(h @ w_gate) * (h @ w_up)) @ w_down


def fourier_mix(u):
    b, s, _ = u.shape
    uh = u.astype(jnp.float32).reshape(b, s, N_F_HEADS, F_HEAD_DIM)
    y = jnp.fft.fft2(uh, axes=(1, 3), norm="ortho").real
    return y.reshape(b, s, D_FOURIER).astype(u.dtype)


def centred_short_conv(v, w):
    vp = jnp.pad(v, ((0, 0), (1, 1), (0, 0)))
    return vp[:, :-2] * w[0] + vp[:, 1:-1] * w[1] + vp[:, 2:] * w[2]


def encoder_layer(x, g_ffn1, w1_gate, w1_up, w1_down, g_mix, w_in, conv_w,
                  g_fourier, g_conv, w_out, g_ffn2, w2_gate, w2_up, w2_down):
    x = x + FFN_RES_SCALE * swiglu(rms_norm(x, g_ffn1), w1_gate, w1_up, w1_down)
    h = rms_norm(x, g_mix)
    u = h @ w_in
    u_f = u[..., :D_FOURIER]
    gate_b, gate_c, v = jnp.split(u[..., D_FOURIER:], 3, axis=-1)
    y_f = rms_norm(fourier_mix(u_f), g_fourier)
    y_c = rms_norm(gate_b * centred_short_conv(gate_c * v, conv_w), g_conv)
    x = x + jnp.concatenate([y_f, y_c], axis=-1) @ w_out
    x = x + FFN_RES_SCALE * swiglu(rms_norm(x, g_ffn2), w2_gate, w2_up, w2_down)
    return x


def setup_inputs(seed: int = 0) -> dict:
    key = jax.random.key(seed)
    ks = jax.random.split(key, 20)
    f32 = jnp.float32

    def nrm(k, shape, scale):
        return jax.random.normal(k, shape, f32) * scale

    def gain(k, n):
        return 1.0 + 0.05 * jax.random.normal(k, (DEPTH, n), f32)

    return {
        "x_prompt": jax.random.normal(ks[0], (BATCH, SEQ, D_MODEL), f32),
        "x_sample": jax.random.normal(ks[1], (DEC_BATCH, DEC_SEQ, D_MODEL), f32),
        "g_ffn1": gain(ks[2], D_MODEL),
        "w1_gate": nrm(ks[3], (DEPTH, D_MODEL, D_FF), D_MODEL ** -0.5),
        "w1_up": nrm(ks[4], (DEPTH, D_MODEL, D_FF), D_MODEL ** -0.5),
        "w1_down": nrm(ks[5], (DEPTH, D_FF, D_MODEL), D_FF ** -0.5),
        "g_mix": gain(ks[6], D_MODEL),
        "w_in": nrm(ks[7], (DEPTH, D_MODEL, D_IN), D_MODEL ** -0.5),
        "conv_w": nrm(ks[8], (DEPTH, CONV_WIDTH, D_CONV), CONV_WIDTH ** -0.5),
        "g_fourier": gain(ks[9], D_FOURIER),
        "g_conv": gain(ks[10], D_CONV),
        "w_out": nrm(ks[11], (DEPTH, D_MIX, D_MODEL), D_MIX ** -0.5),
        "g_ffn2": gain(ks[12], D_MODEL),
        "w2_gate": nrm(ks[13], (DEPTH, D_MODEL, D_FF), D_MODEL ** -0.5),
        "w2_up": nrm(ks[14], (DEPTH, D_MODEL, D_FF), D_MODEL ** -0.5),
        "w2_down": nrm(ks[15], (DEPTH, D_FF, D_MODEL), D_FF ** -0.5),
        "g_final": 1.0 + 0.05 * jax.random.normal(ks[16], (D_MODEL,), f32),
    }


def reference(x_prompt, x_sample, g_ffn1, w1_gate, w1_up, w1_down, g_mix, w_in,
              conv_w, g_fourier, g_conv, w_out, g_ffn2, w2_gate, w2_up, w2_down,
              g_final):
    def trunk(x):
        for l in range(DEPTH):
            x = encoder_layer(x, g_ffn1[l], w1_gate[l], w1_up[l], w1_down[l],
                              g_mix[l], w_in[l], conv_w[l], g_fourier[l], g_conv[l],
                              w_out[l], g_ffn2[l], w2_gate[l], w2_up[l], w2_down[l])
        return rms_norm(x, g_final)

    y_prompt = trunk(x_prompt)
    y_sample = trunk(x_sample)
    return (y_prompt, y_sample)
```

```python
import functools

import numpy as np
import jax
import jax.numpy as jnp
from jax import lax
from jax.experimental import pallas as pl
from jax.experimental.pallas import tpu as pltpu

D_FOURIER = 512
N_F_HEADS = 4
F_HEAD_DIM = D_FOURIER // N_F_HEADS
D_CONV = 512
RMS_EPS = 1e-6
FFN_RES_SCALE = 0.5

VMEM_LIMIT_BIG = 56 * 1024 * 1024
VMEM_LIMIT_SMALL = 40 * 1024 * 1024
SUBLANES = 8

_BF16 = jnp.bfloat16
_F32 = jnp.float32


def _dot(a, b):
    return jnp.dot(a, b, preferred_element_type=_F32)


def _rms(x, g):
    return x * lax.rsqrt(jnp.mean(x * x, axis=-1, keepdims=True) + RMS_EPS) * g


def _const_spec(shape):
    nd = len(shape)
    return pl.BlockSpec(shape, lambda *_: (0,) * nd, pipeline_mode=pl.Buffered(1))


def _swiglu(h, wg_ref, wu_ref, wd_ref, n_chunks):
    d_ff = wg_ref.shape[1]
    fc = d_ff // n_chunks
    acc = None
    for c in range(n_chunks):
        sl = slice(c * fc, (c + 1) * fc)
        g = _dot(h, wg_ref[:, sl])
        u = _dot(h, wu_ref[:, sl])
        a = (g * jax.nn.sigmoid(g) * u).astype(_BF16)
        p = _dot(a, wd_ref[sl, :])
        acc = p if acc is None else acc + p
    return acc


def _ffn1_inproj_body(x_ref, g1_ref, wg_ref, wu_ref, wd_ref, gm_ref, win_ref,
                      x1_ref, uf_ref, gb_ref, cv_ref, *, n_chunks):
    x = x_ref[...]
    h = _rms(x, g1_ref[...]).astype(_BF16)
    x1 = x + FFN_RES_SCALE * _swiglu(h, wg_ref, wu_ref, wd_ref, n_chunks)
    x1_ref[...] = x1
    h2 = _rms(x1, gm_ref[...]).astype(_BF16)
    u = _dot(h2, win_ref[...])
    uf_ref[...] = u[:, :D_FOURIER]
    gb_ref[...] = u[:, D_FOURIER:D_FOURIER + D_CONV].astype(_BF16)
    cv_ref[...] = (u[:, D_FOURIER + D_CONV:D_FOURIER + 2 * D_CONV]
                   * u[:, D_FOURIER + 2 * D_CONV:]).astype(_BF16)


def _ffn1_inproj(x2d, g1, wg, wu, wd, gm, win, *, tm, n_chunks):
    t, d = x2d.shape
    d_ff = wg.shape[1]
    d_in = win.shape[1]
    row = lambda w: pl.BlockSpec((tm, w), lambda i: (i, 0))
    return pl.pallas_call(
        functools.partial(_ffn1_inproj_body, n_chunks=n_chunks),
        grid=(t // tm,),
        in_specs=[row(d), _const_spec((1, d)), _const_spec((d, d_ff)), _const_spec((d, d_ff)),
                  _const_spec((d_ff, d)), _const_spec((1, d)), _const_spec((d, d_in))],
        out_specs=[row(d), row(D_FOURIER), row(D_CONV), row(D_CONV)],
        out_shape=[jax.ShapeDtypeStruct((t, d), _F32),
                   jax.ShapeDtypeStruct((t, D_FOURIER), _F32),
                   jax.ShapeDtypeStruct((t, D_CONV), _BF16),
                   jax.ShapeDtypeStruct((t, D_CONV), _BF16)],
        compiler_params=pltpu.CompilerParams(dimension_semantics=("arbitrary",),
                                             vmem_limit_bytes=VMEM_LIMIT_BIG),
        name="ffn1_inproj",
    )(x2d, g1, wg, wu, wd, gm, win)


def _stage_a_table(n1, n2):
    s = n1 * n2
    k1 = np.arange(n1, dtype=np.int64)[None, :, None]
    i1 = np.arange(n1, dtype=np.int64)[None, None, :]
    i2 = np.arange(n2, dtype=np.int64)[:, None, None]
    ang = 2.0 * np.pi * ((k1 * (n2 * i1 + i2)) % s) / s
    m = np.concatenate([np.cos(ang), -np.sin(ang)], axis=1) / np.sqrt(n1)
    return m.astype(np.float32)


def _stage_b_tables(n2):
    k = np.arange(n2, dtype=np.int64)
    ang = 2.0 * np.pi * ((k[:, None] * k[None, :]) % n2) / n2
    c, s = np.cos(ang) / np.sqrt(n2), np.sin(ang) / np.sqrt(n2)
    return (np.concatenate([c, -s], axis=0).astype(np.float32),
            np.concatenate([s, c], axis=0).astype(np.float32))


def _seqdft_a_body(x_ref, m_ref, o_ref):
    for j in range(SUBLANES):
        xj = x_ref[:, j, :].astype(_BF16)
        o_ref[j] = _dot(m_ref[j], xj)


def _seqdft_a(uf, m_tab, *, n1, n2):
    b, s, c = uf.shape
    x4 = uf.reshape(b, n1, n2, c)
    return pl.pallas_call(
        _seqdft_a_body,
        grid=(n2 // SUBLANES, b),
        in_specs=[pl.BlockSpec((None, n1, SUBLANES, c), lambda j, i: (i, 0, j, 0)),
                  pl.BlockSpec((SUBLANES, 2 * n1, n1), lambda j, i: (j, 0, 0))],
        out_specs=pl.BlockSpec((None, SUBLANES, 2 * n1, c), lambda j, i: (i, j, 0, 0)),
        out_shape=jax.ShapeDtypeStruct((b, n2, 2 * n1, c), _F32),
        compiler_params=pltpu.CompilerParams(dimension_semantics=("arbitrary", "arbitrary"),
                                             vmem_limit_bytes=VMEM_LIMIT_SMALL),
        name="seqdft_a",
    )(x4, m_tab)


def _seqdft_b_body(re_ref, im_ref, la_ref, lb_ref, gr_ref, gi_ref):
    n2 = re_ref.shape[0]
    la = la_ref[...]
    lb = lb_ref[...]
    for j in range(SUBLANES):
        br = re_ref[:, j, :].astype(_BF16)
        bi = im_ref[:, j, :].astype(_BF16)
        r = _dot(la, br) + _dot(lb, bi)
        gr_ref[:, j, :] = r[:n2]
        gi_ref[:, j, :] = r[n2:]


def _seqdft_b(o1, la, lb, *, n1, n2):
    b, _, _, c = o1.shape
    nk = n1 // SUBLANES
    g_spec = pl.BlockSpec((None, n2, SUBLANES, c), lambda j, i: (i, 0, j, 0))
    g_shape = jax.ShapeDtypeStruct((b, n2, n1, c), _F32)
    gr, gi = pl.pallas_call(
        _seqdft_b_body,
        grid=(nk, b),
        in_specs=[pl.BlockSpec((None, n2, SUBLANES, c), lambda j, i: (i, 0, j, 0)),
                  pl.BlockSpec((None, n2, SUBLANES, c), lambda j, i: (i, 0, nk + j, 0)),
                  _const_spec((2 * n2, n2)), _const_spec((2 * n2, n2))],
        out_specs=[g_spec, g_spec],
        out_shape=[g_shape, g_shape],
        compiler_params=pltpu.CompilerParams(dimension_semantics=("arbitrary", "arbitrary"),
                                             vmem_limit_bytes=VMEM_LIMIT_SMALL),
        name="seqdft_b",
    )(o1, o1, la, lb)
    return gr.reshape(b, n2 * n1, c), gi.reshape(b, n2 * n1, c)


def _chan_tables():
    k = np.arange(F_HEAD_DIM, dtype=np.int64)
    ang = 2.0 * np.pi * ((k[:, None] * k[None, :]) % F_HEAD_DIM) / F_HEAD_DIM
    c, s = np.cos(ang) / np.sqrt(F_HEAD_DIM), np.sin(ang) / np.sqrt(F_HEAD_DIM)
    eye2 = np.eye(2)
    return np.kron(eye2, c).astype(np.float32), np.kron(eye2, s).astype(np.float32)


def _mix_ffn2_body(x1_ref, gr_ref, gi_ref, gb_ref, cv_ref, cvp_ref, cvn_ref,
                   cc_ref, sc_ref, cw_ref, gf_ref, gc_ref, wo_ref,
                   g2_ref, wg_ref, wu_ref, wd_ref, gfin_ref, o_ref, *, n_chunks, tiles_per_seq):
    i = pl.program_id(0)
    tm = x1_ref.shape[0]
    pair = 2 * F_HEAD_DIM

    gr = gr_ref[...].astype(_BF16)
    gi = gi_ref[...].astype(_BF16)
    cc = cc_ref[...]
    sc = sc_ref[...]
    yf = jnp.concatenate(
        [_dot(gr[:, p * pair:(p + 1) * pair], cc) + _dot(gi[:, p * pair:(p + 1) * pair], sc)
         for p in range(D_FOURIER // pair)], axis=-1)
    yf = _rms(yf, gf_ref[...])

    cv = cv_ref[...].astype(_F32)
    first = (i % tiles_per_seq) == 0
    last = (i % tiles_per_seq) == tiles_per_seq - 1
    prev_row = jnp.where(first, 0.0, cvp_ref[SUBLANES - 1:SUBLANES, :].astype(_F32))
    next_row = jnp.where(last, 0.0, cvn_ref[0:1, :].astype(_F32))
    rows = lax.broadcasted_iota(jnp.int32, cv.shape, 0)
    cv_m1 = jnp.where(rows == 0, prev_row, pltpu.roll(cv, 1, 0))
    cv_p1 = jnp.where(rows == tm - 1, next_row, pltpu.roll(cv, tm - 1, 0))
    cw = cw_ref[...]
    conv = cv_m1 * cw[0:1, :] + cv * cw[1:2, :] + cv_p1 * cw[2:3, :]
    yc = _rms(gb_ref[...].astype(_F32) * conv, gc_ref[...])

    y = jnp.concatenate([yf, yc], axis=-1).astype(_BF16)
    x2 = x1_ref[...] + _dot(y, wo_ref[...])
    h = _rms(x2, g2_ref[...]).astype(_BF16)
    x3 = x2 + FFN_RES_SCALE * _swiglu(h, wg_ref, wu_ref, wd_ref, n_chunks)
    o_ref[...] = _rms(x3, gfin_ref[...])


def _mix_ffn2(x1, gr, gi, gb, cv, cc, sc, cw, gf, gc, wo, g2, wg, wu, wd, gfin,
              *, tm, n_chunks, seq):
    t, d = x1.shape
    d_ff = wg.shape[1]
    hb = tm // SUBLANES
    n_hblocks = t // SUBLANES
    row = lambda w: pl.BlockSpec((tm, w), lambda i: (i, 0))
    prev_spec = pl.BlockSpec((SUBLANES, D_CONV), lambda i: (jnp.maximum(i * hb - 1, 0), 0))
    next_spec = pl.BlockSpec((SUBLANES, D_CONV),
                             lambda i: (jnp.minimum((i + 1) * hb, n_hblocks - 1), 0))
    return pl.pallas_call(
        functools.partial(_mix_ffn2_body, n_chunks=n_chunks, tiles_per_seq=seq // tm),
        grid=(t // tm,),
        in_specs=[row(d), row(D_FOURIER), row(D_FOURIER), row(D_CONV), row(D_CONV),
                  prev_spec, next_spec,
                  _const_spec(cc.shape), _const_spec(sc.shape), _const_spec(cw.shape),
                  _const_spec((1, D_FOURIER)), _const_spec((1, D_CONV)), _const_spec((d, d)),
                  _const_spec((1, d)), _const_spec((d, d_ff)), _const_spec((d, d_ff)),
                  _const_spec((d_ff, d)), _const_spec((1, d))],
        out_specs=row(d),
        out_shape=jax.ShapeDtypeStruct((t, d), _F32),
        compiler_params=pltpu.CompilerParams(dimension_semantics=("arbitrary",),
                                             vmem_limit_bytes=VMEM_LIMIT_BIG),
        name="mix_ffn2",
    )(x1, gr, gi, gb, cv, cv, cv, cc, sc, cw, gf, gc, wo, g2, wg, wu, wd, gfin)


def _dft_split(s):
    lg = s.bit_length() - 1
    assert 1 << lg == s, "sequence length must be a power of two"
    n2 = 1 << (lg // 2)
    return s // n2, n2


def _trunk(x, p, *, tm, n_chunks):
    b, s, d = x.shape
    n1, n2 = _dft_split(s)
    assert s % tm == 0 and n2 % SUBLANES == 0 and n1 % SUBLANES == 0
    x2d = x.reshape(b * s, d)
    x1, uf, gb, cv = _ffn1_inproj(x2d, p["g1"], p["w1g"], p["w1u"], p["w1d"], p["gm"], p["win"],
                                  tm=tm, n_chunks=n_chunks)
    m_tab = jnp.asarray(_stage_a_table(n1, n2)).astype(_BF16)
    la, lb = (jnp.asarray(a).astype(_BF16) for a in _stage_b_tables(n2))
    o1 = _seqdft_a(uf.reshape(b, s, D_FOURIER), m_tab, n1=n1, n2=n2)
    gr, gi = _seqdft_b(o1, la, lb, n1=n1, n2=n2)
    cc, sc = (jnp.asarray(a).astype(_BF16) for a in _chan_tables())
    out = _mix_ffn2(x1, gr.reshape(b * s, D_FOURIER), gi.reshape(b * s, D_FOURIER), gb, cv,
                    cc, sc, p["cw"], p["gf"], p["gc"], p["wo"], p["g2"], p["w2g"], p["w2u"],
                    p["w2d"], p["gfin"], tm=tm, n_chunks=n_chunks, seq=s)
    return out.reshape(b, s, d)


def _layer_params(l, g_ffn1, w1_gate, w1_up, w1_down, g_mix, w_in, conv_w, g_fourier, g_conv,
                  w_out, g_ffn2, w2_gate, w2_up, w2_down, g_final):
    vec = lambda g: g.reshape(1, -1).astype(_F32)
    return dict(g1=vec(g_ffn1[l]), w1g=w1_gate[l].astype(_BF16), w1u=w1_up[l].astype(_BF16),
                w1d=w1_down[l].astype(_BF16), gm=vec(g_mix[l]), win=w_in[l].astype(_BF16),
                cw=conv_w[l].astype(_F32), gf=vec(g_fourier[l]), gc=vec(g_conv[l]),
                wo=w_out[l].astype(_BF16), g2=vec(g_ffn2[l]), w2g=w2_gate[l].astype(_BF16),
                w2u=w2_up[l].astype(_BF16), w2d=w2_down[l].astype(_BF16), gfin=vec(g_final))


def kernel(x_prompt, x_sample, g_ffn1, w1_gate, w1_up, w1_down, g_mix, w_in, conv_w, g_fourier,
           g_conv, w_out, g_ffn2, w2_gate, w2_up, w2_down, g_final):
    assert g_ffn1.shape[0] == 1, "single-layer trunk"
    p = _layer_params(0, g_ffn1, w1_gate, w1_up, w1_down, g_mix, w_in, conv_w, g_fourier, g_conv,
                      w_out, g_ffn2, w2_gate, w2_up, w2_down, g_final)
    run = functools.partial(_trunk, p=p, tm=512, n_chunks=2)
    return (run(x_prompt), run(x_sample))
```

```python
import functools

import numpy as np
import jax
import jax.numpy as jnp
from jax import lax
from jax.experimental import pallas as pl
from jax.experimental.pallas import tpu as pltpu

D_FOURIER = 512
N_F_HEADS = 4
F_HEAD_DIM = D_FOURIER // N_F_HEADS
D_CONV = 512
RMS_EPS = 1e-6
FFN_RES_SCALE = 0.5

VMEM_LIMIT_BIG = 56 * 1024 * 1024
VMEM_LIMIT_SMALL = 40 * 1024 * 1024
SUBLANES = 8
BF16_ROWS = 16
DFT_N2 = 16
KTILE = BF16_ROWS

_BF16 = jnp.bfloat16
_F32 = jnp.float32


def _dot(a, b):
    return jnp.dot(a, b, preferred_element_type=_F32)


def _rms(x, g):
    return x * lax.rsqrt(jnp.mean(x * x, axis=-1, keepdims=True) + RMS_EPS) * g


def _const_spec(shape):
    nd = len(shape)
    return pl.BlockSpec(shape, lambda *_: (0,) * nd, pipeline_mode=pl.Buffered(1))


def _swiglu(h, wg_ref, wu_ref, wd_ref, n_chunks):
    d_ff = wg_ref.shape[1]
    fc = d_ff // n_chunks
    acc = None
    for c in range(n_chunks):
        sl = slice(c * fc, (c + 1) * fc)
        g = _dot(h, wg_ref[:, sl])
        u = _dot(h, wu_ref[:, sl])
        a = (g * jax.nn.sigmoid(g) * u).astype(_BF16)
        p = _dot(a, wd_ref[sl, :])
        acc = p if acc is None else acc + p
    return acc


def _row_perm(tm, n2):
    m = tm // n2
    p = np.zeros((tm, tm), np.float32)
    src = np.arange(tm)
    p[(src % n2) * m + src // n2, src] = 1.0
    return p


def _ffn1_inproj_body(x_ref, g1_ref, wg_ref, wu_ref, wd_ref, gm_ref, win_ref, perm_ref,
                      x1_ref, uf_ref, gb_ref, cv_ref, *, n_chunks):
    x = x_ref[...]
    h = _rms(x, g1_ref[...]).astype(_BF16)
    x1 = x + FFN_RES_SCALE * _swiglu(h, wg_ref, wu_ref, wd_ref, n_chunks)
    x1_ref[...] = x1
    h2 = _rms(x1, gm_ref[...]).astype(_BF16)
    u = _dot(h2, win_ref[...])
    uf = _dot(perm_ref[...], u[:, :D_FOURIER].astype(_BF16)).astype(_BF16)
    uf_ref[...] = uf.reshape(uf_ref.shape)
    gb_ref[...] = u[:, D_FOURIER:D_FOURIER + D_CONV].astype(_BF16)
    cv_ref[...] = (u[:, D_FOURIER + D_CONV:D_FOURIER + 2 * D_CONV]
                   * u[:, D_FOURIER + 2 * D_CONV:]).astype(_BF16)


def _ffn1_inproj(x2d, g1, wg, wu, wd, gm, win, perm, *, b, s, n1, n2, tm, n_chunks):
    t, d = x2d.shape
    d_ff = wg.shape[1]
    d_in = win.shape[1]
    tps = s // tm
    row = lambda w: pl.BlockSpec((tm, w), lambda i: (i, 0))
    return pl.pallas_call(
        functools.partial(_ffn1_inproj_body, n_chunks=n_chunks),
        grid=(t // tm,),
        in_specs=[row(d), _const_spec((1, d)), _const_spec((d, d_ff)), _const_spec((d, d_ff)),
                  _const_spec((d_ff, d)), _const_spec((1, d)), _const_spec((d, d_in)),
                  _const_spec((tm, tm))],
        out_specs=[row(d),
                   pl.BlockSpec((None, n2, tm // n2, D_FOURIER),
                                lambda i: (i // tps, 0, i % tps, 0)),
                   row(D_CONV), row(D_CONV)],
        out_shape=[jax.ShapeDtypeStruct((t, d), _F32),
                   jax.ShapeDtypeStruct((b, n2, n1, D_FOURIER), _BF16),
                   jax.ShapeDtypeStruct((t, D_CONV), _BF16),
                   jax.ShapeDtypeStruct((t, D_CONV), _BF16)],
        compiler_params=pltpu.CompilerParams(dimension_semantics=("arbitrary",),
                                             vmem_limit_bytes=VMEM_LIMIT_BIG),
        name="ffn1_inproj",
    )(x2d, g1, wg, wu, wd, gm, win, perm)


def _stage_a_table(n1, n2):
    s = n1 * n2
    k = np.arange(n1, dtype=np.int64)
    a1 = 2.0 * np.pi * ((k[:, None] * k[None, :]) % n1) / n1
    a2 = 2.0 * np.pi * ((np.arange(n2, dtype=np.int64)[:, None] * k[None, :]) % s) / s
    scale = 1.0 / np.sqrt(n1)
    c1 = jnp.asarray((np.cos(a1) * scale).astype(np.float32))[None, :, :]
    s1 = jnp.asarray((np.sin(a1) * scale).astype(np.float32))[None, :, :]
    tc = jnp.asarray(np.cos(a2).astype(np.float32))[:, :, None]
    ts = jnp.asarray(np.sin(a2).astype(np.float32))[:, :, None]
    re = c1 * tc - s1 * ts
    im = -(s1 * tc + c1 * ts)
    return jnp.concatenate([re, im], axis=1).astype(_BF16)


def _stage_b_tables(n2, t):
    k = np.arange(n2, dtype=np.int64)
    ang = 2.0 * np.pi * ((k[:, None] * k[None, :]) % n2) / n2
    c, s = np.cos(ang) / np.sqrt(n2), np.sin(ang) / np.sqrt(n2)
    eye = np.eye(t)
    from_re = np.concatenate([np.kron(c, eye), np.kron(-s, eye)], axis=0)
    from_im = np.concatenate([np.kron(s, eye), np.kron(c, eye)], axis=0)
    return from_re.astype(np.float32), from_im.astype(np.float32)


def _seqdft_a_body(x_ref, m_ref, o_ref):
    m = m_ref[...]
    for i in range(x_ref.shape[0]):
        o_ref[i] = _dot(m, x_ref[i]).astype(_BF16).reshape(o_ref.shape[1:])


def _seqdft_a(uf, m_tab, *, bt):
    b, n2, n1, c = uf.shape
    return pl.pallas_call(
        _seqdft_a_body,
        grid=(n2, b // bt),
        in_specs=[pl.BlockSpec((bt, None, n1, c), lambda j, i: (i, j, 0, 0)),
                  pl.BlockSpec((None, 2 * n1, n1), lambda j, i: (j, 0, 0))],
        out_specs=pl.BlockSpec((bt, None, 2, n1, c), lambda j, i: (i, j, 0, 0, 0)),
        out_shape=jax.ShapeDtypeStruct((b, n2, 2, n1, c), _BF16),
        compiler_params=pltpu.CompilerParams(dimension_semantics=("arbitrary", "arbitrary"),
                                             vmem_limit_bytes=VMEM_LIMIT_SMALL),
        name="seqdft_a",
    )(uf, m_tab)


def _seqdft_b_body(x_ref, fr_ref, fi_ref, gr_ref, gi_ref):
    bt, n2, _, kw, c = x_ref.shape
    rows = n2 * KTILE
    fr = fr_ref[...]
    fi = fi_ref[...]
    for i in range(bt):
        for q in range(kw // KTILE):
            sl = slice(q * KTILE, (q + 1) * KTILE)
            xr = x_ref[i, :, 0, sl, :].reshape(rows, c)
            xi = x_ref[i, :, 1, sl, :].reshape(rows, c)
            r = _dot(fr, xr) + _dot(fi, xi)
            gr_ref[i, :, sl, :] = r[:rows].astype(_BF16).reshape(n2, KTILE, c)
            gi_ref[i, :, sl, :] = r[rows:].astype(_BF16).reshape(n2, KTILE, c)


def _seqdft_b(o1, fr, fi, *, bt, kt):
    b, n2, _, n1, c = o1.shape
    kw = kt * KTILE
    g_spec = pl.BlockSpec((bt, n2, kw, c), lambda j, i: (i, 0, j, 0))
    g_shape = jax.ShapeDtypeStruct((b, n2, n1, c), _BF16)
    return pl.pallas_call(
        _seqdft_b_body,
        grid=(n1 // kw, b // bt),
        in_specs=[pl.BlockSpec((bt, n2, 2, kw, c), lambda j, i: (i, 0, 0, j, 0)),
                  _const_spec(fr.shape), _const_spec(fi.shape)],
        out_specs=[g_spec, g_spec],
        out_shape=[g_shape, g_shape],
        compiler_params=pltpu.CompilerParams(dimension_semantics=("arbitrary", "arbitrary"),
                                             vmem_limit_bytes=VMEM_LIMIT_SMALL),
        name="seqdft_b",
    )(o1, fr, fi)


def _chan_tables():
    k = np.arange(F_HEAD_DIM, dtype=np.int64)
    ang = 2.0 * np.pi * ((k[:, None] * k[None, :]) % F_HEAD_DIM) / F_HEAD_DIM
    c, s = np.cos(ang) / np.sqrt(F_HEAD_DIM), np.sin(ang) / np.sqrt(F_HEAD_DIM)
    eye2 = np.eye(2)
    return np.kron(eye2, c).astype(np.float32), np.kron(eye2, s).astype(np.float32)


def _mix_ffn2_body(x1_ref, gr_ref, gi_ref, gb_ref, cv_ref, cvp_ref, cvn_ref,
                   cc_ref, sc_ref, cw_ref, gf_ref, gc_ref, wo_ref,
                   g2_ref, wg_ref, wu_ref, wd_ref, gfin_ref, o_ref, *, n_chunks, tiles_per_seq):
    i = pl.program_id(0)
    tm = x1_ref.shape[0]
    pair = 2 * F_HEAD_DIM

    gr = gr_ref[...]
    gi = gi_ref[...]
    cc = cc_ref[...]
    sc = sc_ref[...]
    yf = jnp.concatenate(
        [_dot(gr[:, p * pair:(p + 1) * pair], cc) + _dot(gi[:, p * pair:(p + 1) * pair], sc)
         for p in range(D_FOURIER // pair)], axis=-1)
    yf = _rms(yf, gf_ref[...])

    cv = cv_ref[...].astype(_F32)
    first = (i % tiles_per_seq) == 0
    last = (i % tiles_per_seq) == tiles_per_seq - 1
    prev_row = jnp.where(first, 0.0, cvp_ref[BF16_ROWS - 1:BF16_ROWS, :].astype(_F32))
    next_row = jnp.where(last, 0.0, cvn_ref[0:1, :].astype(_F32))
    rows = lax.broadcasted_iota(jnp.int32, cv.shape, 0)
    cv_m1 = jnp.where(rows == 0, prev_row, pltpu.roll(cv, 1, 0))
    cv_p1 = jnp.where(rows == tm - 1, next_row, pltpu.roll(cv, tm - 1, 0))
    cw = cw_ref[...]
    conv = cv_m1 * cw[0:1, :] + cv * cw[1:2, :] + cv_p1 * cw[2:3, :]
    yc = _rms(gb_ref[...].astype(_F32) * conv, gc_ref[...])

    y = jnp.concatenate([yf, yc], axis=-1).astype(_BF16)
    x2 = x1_ref[...] + _dot(y, wo_ref[...])
    h = _rms(x2, g2_ref[...]).astype(_BF16)
    x3 = x2 + FFN_RES_SCALE * _swiglu(h, wg_ref, wu_ref, wd_ref, n_chunks)
    o_ref[...] = _rms(x3, gfin_ref[...])


def _mix_ffn2(x1, gr, gi, gb, cv, cc, sc, cw, gf, gc, wo, g2, wg, wu, wd, gfin,
              *, tm, n_chunks, seq):
    t, d = x1.shape
    d_ff = wg.shape[1]
    hb = tm // BF16_ROWS
    n_hblocks = t // BF16_ROWS
    row = lambda w: pl.BlockSpec((tm, w), lambda i: (i, 0))
    prev_spec = pl.BlockSpec((BF16_ROWS, D_CONV), lambda i: (jnp.maximum(i * hb - 1, 0), 0))
    next_spec = pl.BlockSpec((BF16_ROWS, D_CONV),
                             lambda i: (jnp.minimum((i + 1) * hb, n_hblocks - 1), 0))
    return pl.pallas_call(
        functools.partial(_mix_ffn2_body, n_chunks=n_chunks, tiles_per_seq=seq // tm),
        grid=(t // tm,),
        in_specs=[row(d), row(D_FOURIER), row(D_FOURIER), row(D_CONV), row(D_CONV),
                  prev_spec, next_spec,
                  _const_spec(cc.shape), _const_spec(sc.shape), _const_spec(cw.shape),
                  _const_spec((1, D_FOURIER)), _const_spec((1, D_CONV)), _const_spec((d, d)),
                  _const_spec((1, d)), _const_spec((d, d_ff)), _const_spec((d, d_ff)),
                  _const_spec((d_ff, d)), _const_spec((1, d))],
        out_specs=row(d),
        out_shape=jax.ShapeDtypeStruct((t, d), _F32),
        compiler_params=pltpu.CompilerParams(dimension_semantics=("arbitrary",),
                                             vmem_limit_bytes=VMEM_LIMIT_BIG),
        name="mix_ffn2",
    )(x1, gr, gi, gb, cv, cv, cv, cc, sc, cw, gf, gc, wo, g2, wg, wu, wd, gfin)


def _largest_divisor_at_most(n, cap):
    return max(k for k in range(1, cap + 1) if n % k == 0)


def _trunk(x, p, *, tm, n_chunks):
    b, s, d = x.shape
    n2 = DFT_N2
    n1 = s // n2
    assert n1 * n2 == s and s % tm == 0 and tm % (n2 * BF16_ROWS) == 0 and n1 % KTILE == 0
    x2d = x.reshape(b * s, d)
    perm = jnp.asarray(_row_perm(tm, n2)).astype(_BF16)
    x1, uf, gb, cv = _ffn1_inproj(x2d, p["g1"], p["w1g"], p["w1u"], p["w1d"], p["gm"], p["win"],
                                  perm, b=b, s=s, n1=n1, n2=n2, tm=tm, n_chunks=n_chunks)
    bt_a = _largest_divisor_at_most(b, max(1, 2048 // n1))
    o1 = _seqdft_a(uf, _stage_a_table(n1, n2), bt=bt_a)
    fr, fi = (jnp.asarray(a).astype(_BF16) for a in _stage_b_tables(n2, KTILE))
    kt = _largest_divisor_at_most(n1 // KTILE, 4)
    gr, gi = _seqdft_b(o1, fr, fi, bt=_largest_divisor_at_most(b, 2), kt=kt)
    cc, sc = (jnp.asarray(a).astype(_BF16) for a in _chan_tables())
    out = _mix_ffn2(x1, gr.reshape(b * s, D_FOURIER), gi.reshape(b * s, D_FOURIER), gb, cv,
                    cc, sc, p["cw"], p["gf"], p["gc"], p["wo"], p["g2"], p["w2g"], p["w2u"],
                    p["w2d"], p["gfin"], tm=tm, n_chunks=n_chunks, seq=s)
    return out.reshape(b, s, d)


def _layer_params(l, g_ffn1, w1_gate, w1_up, w1_down, g_mix, w_in, conv_w, g_fourier, g_conv,
                  w_out, g_ffn2, w2_gate, w2_up, w2_down, g_final):
    vec = lambda g: g.reshape(1, -1).astype(_F32)
    return dict(g1=vec(g_ffn1[l]), w1g=w1_gate[l].astype(_BF16), w1u=w1_up[l].astype(_BF16),
                w1d=w1_down[l].astype(_BF16), gm=vec(g_mix[l]), win=w_in[l].astype(_BF16),
                cw=conv_w[l].astype(_F32), gf=vec(g_fourier[l]), gc=vec(g_conv[l]),
                wo=w_out[l].astype(_BF16), g2=vec(g_ffn2[l]), w2g=w2_gate[l].astype(_BF16),
                w2u=w2_up[l].astype(_BF16), w2d=w2_down[l].astype(_BF16), gfin=vec(g_final))


def kernel(x_prompt, x_sample, g_ffn1, w1_gate, w1_up, w1_down, g_mix, w_in, conv_w, g_fourier,
           g_conv, w_out, g_ffn2, w2_gate, w2_up, w2_down, g_final):
    assert g_ffn1.shape[0] == 1, "single-layer trunk"
    p = _layer_params(0, g_ffn1, w1_gate, w1_up, w1_down, g_mix, w_in, conv_w, g_fourier, g_conv,
                      w_out, g_ffn2, w2_gate, w2_up, w2_down, g_final)
    run = functools.partial(_trunk, p=p, tm=512, n_chunks=2)
    return (run(x_prompt), run(x_sample))
```

```python
import functools

import numpy as np
import jax
import jax.numpy as jnp
from jax import lax
from jax.experimental import pallas as pl
from jax.experimental.pallas import tpu as pltpu

D_FOURIER = 512
N_F_HEADS = 4
F_HEAD_DIM = D_FOURIER // N_F_HEADS
D_CONV = 512
RMS_EPS = 1e-6
FFN_RES_SCALE = 0.5

VMEM_LIMIT_BIG = 56 * 1024 * 1024
VMEM_LIMIT_SMALL = 40 * 1024 * 1024
SUBLANES = 8
BF16_ROWS = 16
MXU_COLS = 256
DFT_N2 = 16
KTILE = BF16_ROWS

_BF16 = jnp.bfloat16
_F32 = jnp.float32


def _dot(a, b):
    return jnp.dot(a, b, preferred_element_type=_F32)


def _rms(x, g):
    return x * lax.rsqrt(jnp.mean(x * x, axis=-1, keepdims=True) + RMS_EPS) * g


def _const_spec(shape):
    nd = len(shape)
    return pl.BlockSpec(shape, lambda *_: (0,) * nd, pipeline_mode=pl.Buffered(1))


def _chunk_bounds(width, n_chunks):
    tiles = -(-width // MXU_COLS)
    cuts = [min(width, MXU_COLS * ((tiles * c + n_chunks - 1) // n_chunks))
            for c in range(n_chunks + 1)]
    return [(lo, hi) for lo, hi in zip(cuts[:-1], cuts[1:]) if hi > lo]


def _swiglu(h, wg_ref, wu_ref, wd_ref, n_chunks):
    acc = None
    for lo, hi in _chunk_bounds(wg_ref.shape[1], n_chunks):
        sl = slice(lo, hi)
        g = _dot(h, wg_ref[:, sl])
        u = _dot(h, wu_ref[:, sl])
        a = (g * jax.nn.sigmoid(g) * u).astype(_BF16)
        p = _dot(a, wd_ref[sl, :])
        acc = p if acc is None else acc + p
    return acc


def _row_perm(tm, n2):
    m = tm // n2
    p = np.zeros((tm, tm), np.float32)
    src = np.arange(tm)
    p[(src % n2) * m + src // n2, src] = 1.0
    return p


def _ffn1_inproj_body(x_ref, g1_ref, wg_ref, wu_ref, wd_ref, gm_ref, win_ref, perm_ref,
                      x1_ref, uf_ref, gb_ref, cv_ref, *, n_chunks):
    x = x_ref[...]
    h = _rms(x, g1_ref[...]).astype(_BF16)
    x1 = x + FFN_RES_SCALE * _swiglu(h, wg_ref, wu_ref, wd_ref, n_chunks)
    x1_ref[...] = x1
    h2 = _rms(x1, gm_ref[...]).astype(_BF16)
    u = _dot(h2, win_ref[...])
    uf = _dot(perm_ref[...], u[:, :D_FOURIER].astype(_BF16)).astype(_BF16)
    uf_ref[...] = uf.reshape(uf_ref.shape)
    gb_ref[...] = u[:, D_FOURIER:D_FOURIER + D_CONV].astype(_BF16)
    cv_ref[...] = (u[:, D_FOURIER + D_CONV:D_FOURIER + 2 * D_CONV]
                   * u[:, D_FOURIER + 2 * D_CONV:]).astype(_BF16)


def _ffn1_inproj(x2d, g1, wg, wu, wd, gm, win, perm, *, b, s, n1, n2, tm, n_chunks):
    t, d = x2d.shape
    d_ff = wg.shape[1]
    d_in = win.shape[1]
    tps = s // tm
    row = lambda w: pl.BlockSpec((tm, w), lambda i: (i, 0))
    return pl.pallas_call(
        functools.partial(_ffn1_inproj_body, n_chunks=n_chunks),
        grid=(t // tm,),
        in_specs=[row(d), _const_spec((1, d)), _const_spec((d, d_ff)), _const_spec((d, d_ff)),
                  _const_spec((d_ff, d)), _const_spec((1, d)), _const_spec((d, d_in)),
                  _const_spec((tm, tm))],
        out_specs=[row(d),
                   pl.BlockSpec((None, n2, tm // n2, D_FOURIER),
                                lambda i: (i // tps, 0, i % tps, 0)),
                   row(D_CONV), row(D_CONV)],
        out_shape=[jax.ShapeDtypeStruct((t, d), _F32),
                   jax.ShapeDtypeStruct((b, n2, n1, D_FOURIER), _BF16),
                   jax.ShapeDtypeStruct((t, D_CONV), _BF16),
                   jax.ShapeDtypeStruct((t, D_CONV), _BF16)],
        compiler_params=pltpu.CompilerParams(dimension_semantics=("arbitrary",),
                                             vmem_limit_bytes=VMEM_LIMIT_BIG),
        name="ffn1_inproj",
    )(x2d, g1, wg, wu, wd, gm, win, perm)


def _stage_a_table(n1, n2):
    s = n1 * n2
    k = np.arange(n1, dtype=np.int64)
    a1 = 2.0 * np.pi * ((k[:, None] * k[None, :]) % n1) / n1
    a2 = 2.0 * np.pi * ((np.arange(n2, dtype=np.int64)[:, None] * k[None, :]) % s) / s
    scale = 1.0 / np.sqrt(n1)
    c1 = jnp.asarray((np.cos(a1) * scale).astype(np.float32))[None, :, :]
    s1 = jnp.asarray((np.sin(a1) * scale).astype(np.float32))[None, :, :]
    tc = jnp.asarray(np.cos(a2).astype(np.float32))[:, :, None]
    ts = jnp.asarray(np.sin(a2).astype(np.float32))[:, :, None]
    re = c1 * tc - s1 * ts
    im = -(s1 * tc + c1 * ts)
    return jnp.concatenate([re, im], axis=1).astype(_BF16)


def _stage_b_tables(n2, t):
    k = np.arange(n2, dtype=np.int64)
    ang = 2.0 * np.pi * ((k[:, None] * k[None, :]) % n2) / n2
    c, s = np.cos(ang) / np.sqrt(n2), np.sin(ang) / np.sqrt(n2)
    eye = np.eye(t)
    from_re = np.concatenate([np.kron(c, eye), np.kron(-s, eye)], axis=0)
    from_im = np.concatenate([np.kron(s, eye), np.kron(c, eye)], axis=0)
    return from_re.astype(np.float32), from_im.astype(np.float32)


def _seqdft_a_body(x_ref, m_ref, o_ref):
    m = m_ref[...]
    for i in range(x_ref.shape[0]):
        o_ref[i] = _dot(m, x_ref[i]).astype(_BF16).reshape(o_ref.shape[1:])


def _seqdft_a(uf, m_tab, *, bt):
    b, n2, n1, c = uf.shape
    return pl.pallas_call(
        _seqdft_a_body,
        grid=(n2, b // bt),
        in_specs=[pl.BlockSpec((bt, None, n1, c), lambda j, i: (i, j, 0, 0)),
                  pl.BlockSpec((None, 2 * n1, n1), lambda j, i: (j, 0, 0))],
        out_specs=pl.BlockSpec((bt, None, 2, n1, c), lambda j, i: (i, j, 0, 0, 0)),
        out_shape=jax.ShapeDtypeStruct((b, n2, 2, n1, c), _BF16),
        compiler_params=pltpu.CompilerParams(dimension_semantics=("arbitrary", "arbitrary"),
                                             vmem_limit_bytes=VMEM_LIMIT_SMALL),
        name="seqdft_a",
    )(uf, m_tab)


def _chan_tables():
    k = np.arange(F_HEAD_DIM, dtype=np.int64)
    ang = 2.0 * np.pi * ((k[:, None] * k[None, :]) % F_HEAD_DIM) / F_HEAD_DIM
    c, s = np.cos(ang) / np.sqrt(F_HEAD_DIM), np.sin(ang) / np.sqrt(F_HEAD_DIM)
    eye2 = np.eye(2)
    return np.kron(eye2, c).astype(np.float32), np.kron(eye2, s).astype(np.float32)


def _seqdft_b_body(x_ref, fr_ref, fi_ref, cc_ref, sc_ref, gf_ref, y_ref):
    bt, n2, _, kw, c = x_ref.shape
    rows = n2 * KTILE
    pair = 2 * F_HEAD_DIM
    fr = fr_ref[...]
    fi = fi_ref[...]
    cc = cc_ref[...]
    sc = sc_ref[...]
    gf = gf_ref[...]
    for i in range(bt):
        for q in range(kw // KTILE):
            sl = slice(q * KTILE, (q + 1) * KTILE)
            xr = x_ref[i, :, 0, sl, :].reshape(rows, c)
            xi = x_ref[i, :, 1, sl, :].reshape(rows, c)
            g = (_dot(fr, xr) + _dot(fi, xi)).astype(_BF16)
            gr = g[:rows]
            gi = g[rows:]
            yf = jnp.concatenate(
                [_dot(gr[:, p * pair:(p + 1) * pair], cc) + _dot(gi[:, p * pair:(p + 1) * pair], sc)
                 for p in range(c // pair)], axis=-1)
            y_ref[i, :, sl, :] = _rms(yf, gf).astype(_BF16).reshape(n2, KTILE, c)


def _seqdft_b(o1, fr, fi, cc, sc, gf, *, bt, kt):
    b, n2, _, n1, c = o1.shape
    kw = kt * KTILE
    return pl.pallas_call(
        _seqdft_b_body,
        grid=(n1 // kw, b // bt),
        in_specs=[pl.BlockSpec((bt, n2, 2, kw, c), lambda j, i: (i, 0, 0, j, 0)),
                  _const_spec(fr.shape), _const_spec(fi.shape),
                  _const_spec(cc.shape), _const_spec(sc.shape), _const_spec(gf.shape)],
        out_specs=pl.BlockSpec((bt, n2, kw, c), lambda j, i: (i, 0, j, 0)),
        out_shape=jax.ShapeDtypeStruct((b, n2, n1, c), _BF16),
        compiler_params=pltpu.CompilerParams(dimension_semantics=("arbitrary", "arbitrary"),
                                             vmem_limit_bytes=VMEM_LIMIT_SMALL),
        name="seqdft_b",
    )(o1, fr, fi, cc, sc, gf)


def _mix_ffn2_body(x1_ref, yf_ref, gb_ref, cv_ref, cvp_ref, cvn_ref, cw_ref, gc_ref, wo_ref,
                   g2_ref, wg_ref, wu_ref, wd_ref, gfin_ref, o_ref, *, n_chunks, tiles_per_seq):
    i = pl.program_id(0)
    tm = x1_ref.shape[0]

    cv = cv_ref[...].astype(_F32)
    first = (i % tiles_per_seq) == 0
    last = (i % tiles_per_seq) == tiles_per_seq - 1
    prev_row = jnp.where(first, 0.0, cvp_ref[BF16_ROWS - 1:BF16_ROWS, :].astype(_F32))
    next_row = jnp.where(last, 0.0, cvn_ref[0:1, :].astype(_F32))
    rows = lax.broadcasted_iota(jnp.int32, cv.shape, 0)
    cv_m1 = jnp.where(rows == 0, prev_row, pltpu.roll(cv, 1, 0))
    cv_p1 = jnp.where(rows == tm - 1, next_row, pltpu.roll(cv, tm - 1, 0))
    cw = cw_ref[...]
    conv = cv_m1 * cw[0:1, :] + cv * cw[1:2, :] + cv_p1 * cw[2:3, :]
    yc = _rms(gb_ref[...].astype(_F32) * conv, gc_ref[...])

    y = jnp.concatenate([yf_ref[...], yc.astype(_BF16)], axis=-1)
    x2 = x1_ref[...] + _dot(y, wo_ref[...])
    h = _rms(x2, g2_ref[...]).astype(_BF16)
    x3 = x2 + FFN_RES_SCALE * _swiglu(h, wg_ref, wu_ref, wd_ref, n_chunks)
    o_ref[...] = _rms(x3, gfin_ref[...])


def _mix_ffn2(x1, yf, gb, cv, cw, gc, wo, g2, wg, wu, wd, gfin, *, tm, n_chunks, seq):
    t, d = x1.shape
    d_ff = wg.shape[1]
    hb = tm // BF16_ROWS
    n_hblocks = t // BF16_ROWS
    row = lambda w: pl.BlockSpec((tm, w), lambda i: (i, 0))
    prev_spec = pl.BlockSpec((BF16_ROWS, D_CONV), lambda i: (jnp.maximum(i * hb - 1, 0), 0))
    next_spec = pl.BlockSpec((BF16_ROWS, D_CONV),
                             lambda i: (jnp.minimum((i + 1) * hb, n_hblocks - 1), 0))
    return pl.pallas_call(
        functools.partial(_mix_ffn2_body, n_chunks=n_chunks, tiles_per_seq=seq // tm),
        grid=(t // tm,),
        in_specs=[row(d), row(D_FOURIER), row(D_CONV), row(D_CONV), prev_spec, next_spec,
                  _const_spec(cw.shape), _const_spec((1, D_CONV)), _const_spec((d, d)),
                  _const_spec((1, d)), _const_spec((d, d_ff)), _const_spec((d, d_ff)),
                  _const_spec((d_ff, d)), _const_spec((1, d))],
        out_specs=row(d),
        out_shape=jax.ShapeDtypeStruct((t, d), _F32),
        compiler_params=pltpu.CompilerParams(dimension_semantics=("arbitrary",),
                                             vmem_limit_bytes=VMEM_LIMIT_BIG),
        name="mix_ffn2",
    )(x1, yf, gb, cv, cv, cv, cw, gc, wo, g2, wg, wu, wd, gfin)


def _largest_divisor_at_most(n, cap):
    return max(k for k in range(1, cap + 1) if n % k == 0)


def _trunk(x, p, *, tm, n_chunks):
    b, s, d = x.shape
    n2 = DFT_N2
    n1 = s // n2
    assert n1 * n2 == s and s % tm == 0 and tm % (n2 * BF16_ROWS) == 0 and n1 % KTILE == 0
    x2d = x.reshape(b * s, d)
    perm = jnp.asarray(_row_perm(tm, n2)).astype(_BF16)
    x1, uf, gb, cv = _ffn1_inproj(x2d, p["g1"], p["w1g"], p["w1u"], p["w1d"], p["gm"], p["win"],
                                  perm, b=b, s=s, n1=n1, n2=n2, tm=tm, n_chunks=n_chunks)
    bt_a = _largest_divisor_at_most(b, max(1, 2048 // n1))
    o1 = _seqdft_a(uf, _stage_a_table(n1, n2), bt=bt_a)
    fr, fi = (jnp.asarray(a).astype(_BF16) for a in _stage_b_tables(n2, KTILE))
    cc, sc = (jnp.asarray(a).astype(_BF16) for a in _chan_tables())
    kt = _largest_divisor_at_most(n1 // KTILE, 4)
    yf = _seqdft_b(o1, fr, fi, cc, sc, p["gf"], bt=_largest_divisor_at_most(b, 2), kt=kt)
    out = _mix_ffn2(x1, yf.reshape(b * s, D_FOURIER), gb, cv, p["cw"], p["gc"], p["wo"], p["g2"],
                    p["w2g"], p["w2u"], p["w2d"], p["gfin"], tm=tm, n_chunks=n_chunks, seq=s)
    return out.reshape(b, s, d)


def _layer_params(l, g_ffn1, w1_gate, w1_up, w1_down, g_mix, w_in, conv_w, g_fourier, g_conv,
                  w_out, g_ffn2, w2_gate, w2_up, w2_down, g_final):
    vec = lambda g: g.reshape(1, -1).astype(_F32)
    return dict(g1=vec(g_ffn1[l]), w1g=w1_gate[l].astype(_BF16), w1u=w1_up[l].astype(_BF16),
                w1d=w1_down[l].astype(_BF16), gm=vec(g_mix[l]), win=w_in[l].astype(_BF16),
                cw=conv_w[l].astype(_F32), gf=vec(g_fourier[l]), gc=vec(g_conv[l]),
                wo=w_out[l].astype(_BF16), g2=vec(g_ffn2[l]), w2g=w2_gate[l].astype(_BF16),
                w2u=w2_up[l].astype(_BF16), w2d=w2_down[l].astype(_BF16), gfin=vec(g_final))


def kernel(x_prompt, x_sample, g_ffn1, w1_gate, w1_up, w1_down, g_mix, w_in, conv_w, g_fourier,
           g_conv, w_out, g_ffn2, w2_gate, w2_up, w2_down, g_final):
    assert g_ffn1.shape[0] == 1, "single-layer trunk"
    p = _layer_params(0, g_ffn1, w1_gate, w1_up, w1_down, g_mix, w_in, conv_w, g_fourier, g_conv,
                      w_out, g_ffn2, w2_gate, w2_up, w2_down, g_final)
    run = functools.partial(_trunk, p=p, tm=512, n_chunks=2)
    return (run(x_prompt), run(x_sample))
```

```python
import functools

import numpy as np
import jax
import jax.numpy as jnp
from jax import lax
from jax.experimental import pallas as pl
from jax.experimental.pallas import tpu as pltpu

D_FOURIER = 512
N_F_HEADS = 4
F_HEAD_DIM = D_FOURIER // N_F_HEADS
D_CONV = 512
RMS_EPS = 1e-6
FFN_RES_SCALE = 0.5

VMEM_LIMIT_BIG = 56 * 1024 * 1024
VMEM_LIMIT_SMALL = 40 * 1024 * 1024
SUBLANES = 8
BF16_ROWS = 16
MXU_COLS = 256
DFT_N2 = 16
KTILE = BF16_ROWS
PERM_ROWS = 512

_BF16 = jnp.bfloat16
_F32 = jnp.float32


def _dot(a, b):
    return jnp.dot(a, b, preferred_element_type=_F32)


def _rms(x, g):
    return x * lax.rsqrt(jnp.mean(x * x, axis=-1, keepdims=True) + RMS_EPS) * g


def _const_spec(shape):
    nd = len(shape)
    return pl.BlockSpec(shape, lambda *_: (0,) * nd, pipeline_mode=pl.Buffered(1))


def _chunk_bounds(width, n_chunks):
    tiles = -(-width // MXU_COLS)
    cuts = [min(width, MXU_COLS * ((tiles * c + n_chunks - 1) // n_chunks))
            for c in range(n_chunks + 1)]
    return [(lo, hi) for lo, hi in zip(cuts[:-1], cuts[1:]) if hi > lo]


def _swiglu(h, wg_ref, wu_ref, wd_ref, n_chunks):
    acc = None
    for lo, hi in _chunk_bounds(wg_ref.shape[1], n_chunks):
        sl = slice(lo, hi)
        g = _dot(h, wg_ref[:, sl])
        u = _dot(h, wu_ref[:, sl])
        a = (g * jax.nn.sigmoid(g) * u).astype(_BF16)
        p = _dot(a, wd_ref[sl, :])
        acc = p if acc is None else acc + p
    return acc


def _row_perm(tm, n2):
    m = tm // n2
    p = np.zeros((tm, tm), np.float32)
    src = np.arange(tm)
    p[(src % n2) * m + src // n2, src] = 1.0
    return p


def _ffn1_inproj_body(x_ref, g1_ref, wg_ref, wu_ref, wd_ref, gm_ref, win_ref, perm_ref,
                      x1_ref, uf_ref, gb_ref, cv_ref, *, n_chunks):
    x = x_ref[...]
    h = _rms(x, g1_ref[...]).astype(_BF16)
    x1 = x + FFN_RES_SCALE * _swiglu(h, wg_ref, wu_ref, wd_ref, n_chunks)
    x1_ref[...] = x1
    h2 = _rms(x1, gm_ref[...]).astype(_BF16)
    u = _dot(h2, win_ref[...])
    n2, m = uf_ref.shape[0], PERM_ROWS // uf_ref.shape[0]
    perm = perm_ref[...]
    for q in range(x.shape[0] // PERM_ROWS):
        uq = u[q * PERM_ROWS:(q + 1) * PERM_ROWS, :D_FOURIER].astype(_BF16)
        uf_ref[:, q * m:(q + 1) * m, :] = _dot(perm, uq).astype(_BF16).reshape(n2, m, D_FOURIER)
    gb_ref[...] = u[:, D_FOURIER:D_FOURIER + D_CONV].astype(_BF16)
    cv_ref[...] = (u[:, D_FOURIER + D_CONV:D_FOURIER + 2 * D_CONV]
                   * u[:, D_FOURIER + 2 * D_CONV:]).astype(_BF16)


def _ffn1_inproj(x2d, g1, wg, wu, wd, gm, win, perm, *, b, s, n1, n2, tm, n_chunks):
    t, d = x2d.shape
    d_ff = wg.shape[1]
    d_in = win.shape[1]
    tps = s // tm
    row = lambda w: pl.BlockSpec((tm, w), lambda i: (i, 0))
    return pl.pallas_call(
        functools.partial(_ffn1_inproj_body, n_chunks=n_chunks),
        grid=(t // tm,),
        in_specs=[row(d), _const_spec((1, d)), _const_spec((d, d_ff)), _const_spec((d, d_ff)),
                  _const_spec((d_ff, d)), _const_spec((1, d)), _const_spec((d, d_in)),
                  _const_spec((PERM_ROWS, PERM_ROWS))],
        out_specs=[row(d),
                   pl.BlockSpec((None, n2, tm // n2, D_FOURIER),
                                lambda i: (i // tps, 0, i % tps, 0)),
                   row(D_CONV), row(D_CONV)],
        out_shape=[jax.ShapeDtypeStruct((t, d), _F32),
                   jax.ShapeDtypeStruct((b, n2, n1, D_FOURIER), _BF16),
                   jax.ShapeDtypeStruct((t, D_CONV), _BF16),
                   jax.ShapeDtypeStruct((t, D_CONV), _BF16)],
        compiler_params=pltpu.CompilerParams(dimension_semantics=("arbitrary",),
                                             vmem_limit_bytes=VMEM_LIMIT_BIG),
        name="ffn1_inproj",
    )(x2d, g1, wg, wu, wd, gm, win, perm)


def _stage_a_table(n1, n2):
    s = n1 * n2
    k = np.arange(n1, dtype=np.int64)
    a1 = 2.0 * np.pi * ((k[:, None] * k[None, :]) % n1) / n1
    a2 = 2.0 * np.pi * ((np.arange(n2, dtype=np.int64)[:, None] * k[None, :]) % s) / s
    scale = 1.0 / np.sqrt(n1)
    c1 = jnp.asarray((np.cos(a1) * scale).astype(np.float32))[None, :, :]
    s1 = jnp.asarray((np.sin(a1) * scale).astype(np.float32))[None, :, :]
    tc = jnp.asarray(np.cos(a2).astype(np.float32))[:, :, None]
    ts = jnp.asarray(np.sin(a2).astype(np.float32))[:, :, None]
    re = c1 * tc - s1 * ts
    im = -(s1 * tc + c1 * ts)
    return jnp.concatenate([re, im], axis=1).astype(_BF16)


def _stage_b_tables(n2, t):
    k = np.arange(n2, dtype=np.int64)
    ang = 2.0 * np.pi * ((k[:, None] * k[None, :]) % n2) / n2
    c, s = np.cos(ang) / np.sqrt(n2), np.sin(ang) / np.sqrt(n2)
    eye = np.eye(t)
    from_re = np.concatenate([np.kron(c, eye), np.kron(-s, eye)], axis=0)
    from_im = np.concatenate([np.kron(s, eye), np.kron(c, eye)], axis=0)
    return from_re.astype(np.float32), from_im.astype(np.float32)


def _seqdft_a_body(x_ref, m_ref, o_ref):
    m = m_ref[...]
    for i in range(x_ref.shape[0]):
        o_ref[i] = _dot(m, x_ref[i]).astype(_BF16).reshape(o_ref.shape[1:])


def _seqdft_a(uf, m_tab, *, bt):
    b, n2, n1, c = uf.shape
    return pl.pallas_call(
        _seqdft_a_body,
        grid=(n2, b // bt),
        in_specs=[pl.BlockSpec((bt, None, n1, c), lambda j, i: (i, j, 0, 0)),
                  pl.BlockSpec((None, 2 * n1, n1), lambda j, i: (j, 0, 0))],
        out_specs=pl.BlockSpec((bt, None, 2, n1, c), lambda j, i: (i, j, 0, 0, 0)),
        out_shape=jax.ShapeDtypeStruct((b, n2, 2, n1, c), _BF16),
        compiler_params=pltpu.CompilerParams(dimension_semantics=("arbitrary", "arbitrary"),
                                             vmem_limit_bytes=VMEM_LIMIT_SMALL),
        name="seqdft_a",
    )(uf, m_tab)


def _chan_tables():
    k = np.arange(F_HEAD_DIM, dtype=np.int64)
    ang = 2.0 * np.pi * ((k[:, None] * k[None, :]) % F_HEAD_DIM) / F_HEAD_DIM
    c, s = np.cos(ang) / np.sqrt(F_HEAD_DIM), np.sin(ang) / np.sqrt(F_HEAD_DIM)
    eye2 = np.eye(2)
    return np.kron(eye2, c).astype(np.float32), np.kron(eye2, s).astype(np.float32)


def _seqdft_b_body(x_ref, fr_ref, fi_ref, cc_ref, sc_ref, gf_ref, y_ref):
    bt, n2, _, kw, c = x_ref.shape
    rows = n2 * KTILE
    pair = 2 * F_HEAD_DIM
    fr = fr_ref[...]
    fi = fi_ref[...]
    cc = cc_ref[...]
    sc = sc_ref[...]
    gf = gf_ref[...]
    for i in range(bt):
        for q in range(kw // KTILE):
            sl = slice(q * KTILE, (q + 1) * KTILE)
            xr = x_ref[i, :, 0, sl, :].reshape(rows, c)
            xi = x_ref[i, :, 1, sl, :].reshape(rows, c)
            g = (_dot(fr, xr) + _dot(fi, xi)).astype(_BF16)
            gr = g[:rows]
            gi = g[rows:]
            yf = jnp.concatenate(
                [_dot(gr[:, p * pair:(p + 1) * pair], cc) + _dot(gi[:, p * pair:(p + 1) * pair], sc)
                 for p in range(c // pair)], axis=-1)
            y_ref[i, :, sl, :] = _rms(yf, gf).astype(_BF16).reshape(n2, KTILE, c)


def _seqdft_b(o1, fr, fi, cc, sc, gf, *, bt, kt):
    b, n2, _, n1, c = o1.shape
    kw = kt * KTILE
    return pl.pallas_call(
        _seqdft_b_body,
        grid=(n1 // kw, b // bt),
        in_specs=[pl.BlockSpec((bt, n2, 2, kw, c), lambda j, i: (i, 0, 0, j, 0)),
                  _const_spec(fr.shape), _const_spec(fi.shape),
                  _const_spec(cc.shape), _const_spec(sc.shape), _const_spec(gf.shape)],
        out_specs=pl.BlockSpec((bt, n2, kw, c), lambda j, i: (i, 0, j, 0)),
        out_shape=jax.ShapeDtypeStruct((b, n2, n1, c), _BF16),
        compiler_params=pltpu.CompilerParams(dimension_semantics=("arbitrary", "arbitrary"),
                                             vmem_limit_bytes=VMEM_LIMIT_SMALL),
        name="seqdft_b",
    )(o1, fr, fi, cc, sc, gf)


def _mix_ffn2_body(x1_ref, yf_ref, gb_ref, cv_ref, cvp_ref, cvn_ref, cw_ref, gc_ref, wo_ref,
                   g2_ref, wg_ref, wu_ref, wd_ref, gfin_ref, o_ref, *, n_chunks, tiles_per_seq):
    i = pl.program_id(0)
    tm = x1_ref.shape[0]

    cv = cv_ref[...].astype(_F32)
    first = (i % tiles_per_seq) == 0
    last = (i % tiles_per_seq) == tiles_per_seq - 1
    prev_row = jnp.where(first, 0.0, cvp_ref[BF16_ROWS - 1:BF16_ROWS, :].astype(_F32))
    next_row = jnp.where(last, 0.0, cvn_ref[0:1, :].astype(_F32))
    rows = lax.broadcasted_iota(jnp.int32, cv.shape, 0)
    cv_m1 = jnp.where(rows == 0, prev_row, pltpu.roll(cv, 1, 0))
    cv_p1 = jnp.where(rows == tm - 1, next_row, pltpu.roll(cv, tm - 1, 0))
    cw = cw_ref[...]
    conv = cv_m1 * cw[0:1, :] + cv * cw[1:2, :] + cv_p1 * cw[2:3, :]
    yc = _rms(gb_ref[...].astype(_F32) * conv, gc_ref[...])

    y = jnp.concatenate([yf_ref[...], yc.astype(_BF16)], axis=-1)
    x2 = x1_ref[...] + _dot(y, wo_ref[...])
    h = _rms(x2, g2_ref[...]).astype(_BF16)
    x3 = x2 + FFN_RES_SCALE * _swiglu(h, wg_ref, wu_ref, wd_ref, n_chunks)
    o_ref[...] = _rms(x3, gfin_ref[...])


def _mix_ffn2(x1, yf, gb, cv, cw, gc, wo, g2, wg, wu, wd, gfin, *, tm, n_chunks, seq):
    t, d = x1.shape
    d_ff = wg.shape[1]
    hb = tm // BF16_ROWS
    n_hblocks = t // BF16_ROWS
    row = lambda w: pl.BlockSpec((tm, w), lambda i: (i, 0))
    prev_spec = pl.BlockSpec((BF16_ROWS, D_CONV), lambda i: (jnp.maximum(i * hb - 1, 0), 0))
    next_spec = pl.BlockSpec((BF16_ROWS, D_CONV),
                             lambda i: (jnp.minimum((i + 1) * hb, n_hblocks - 1), 0))
    return pl.pallas_call(
        functools.partial(_mix_ffn2_body, n_chunks=n_chunks, tiles_per_seq=seq // tm),
        grid=(t // tm,),
        in_specs=[row(d), row(D_FOURIER), row(D_CONV), row(D_CONV), prev_spec, next_spec,
                  _const_spec(cw.shape), _const_spec((1, D_CONV)), _const_spec((d, d)),
                  _const_spec((1, d)), _const_spec((d, d_ff)), _const_spec((d, d_ff)),
                  _const_spec((d_ff, d)), _const_spec((1, d))],
        out_specs=row(d),
        out_shape=jax.ShapeDtypeStruct((t, d), _F32),
        compiler_params=pltpu.CompilerParams(dimension_semantics=("arbitrary",),
                                             vmem_limit_bytes=VMEM_LIMIT_BIG),
        name="mix_ffn2",
    )(x1, yf, gb, cv, cv, cv, cw, gc, wo, g2, wg, wu, wd, gfin)


def _largest_divisor_at_most(n, cap):
    return max(k for k in range(1, cap + 1) if n % k == 0)


def _trunk(x, p, *, tm, n_chunks):
    b, s, d = x.shape
    n2 = DFT_N2
    n1 = s // n2
    assert n1 * n2 == s and s % tm == 0 and tm % PERM_ROWS == 0 and n1 % KTILE == 0
    assert PERM_ROWS % (n2 * BF16_ROWS) == 0
    x2d = x.reshape(b * s, d)
    perm = jnp.asarray(_row_perm(PERM_ROWS, n2)).astype(_BF16)
    x1, uf, gb, cv = _ffn1_inproj(x2d, p["g1"], p["w1g"], p["w1u"], p["w1d"], p["gm"], p["win"],
                                  perm, b=b, s=s, n1=n1, n2=n2, tm=tm, n_chunks=n_chunks)
    bt_a = _largest_divisor_at_most(b, max(1, 2048 // n1))
    o1 = _seqdft_a(uf, _stage_a_table(n1, n2), bt=bt_a)
    fr, fi = (jnp.asarray(a).astype(_BF16) for a in _stage_b_tables(n2, KTILE))
    cc, sc = (jnp.asarray(a).astype(_BF16) for a in _chan_tables())
    kt = _largest_divisor_at_most(n1 // KTILE, 4)
    yf = _seqdft_b(o1, fr, fi, cc, sc, p["gf"], bt=_largest_divisor_at_most(b, 2), kt=kt)
    out = _mix_ffn2(x1, yf.reshape(b * s, D_FOURIER), gb, cv, p["cw"], p["gc"], p["wo"], p["g2"],
                    p["w2g"], p["w2u"], p["w2d"], p["gfin"], tm=tm, n_chunks=n_chunks, seq=s)
    return out.reshape(b, s, d)


def _layer_params(l, g_ffn1, w1_gate, w1_up, w1_down, g_mix, w_in, conv_w, g_fourier, g_conv,
                  w_out, g_ffn2, w2_gate, w2_up, w2_down, g_final):
    vec = lambda g: g.reshape(1, -1).astype(_F32)
    return dict(g1=vec(g_ffn1[l]), w1g=w1_gate[l].astype(_BF16), w1u=w1_up[l].astype(_BF16),
                w1d=w1_down[l].astype(_BF16), gm=vec(g_mix[l]), win=w_in[l].astype(_BF16),
                cw=conv_w[l].astype(_F32), gf=vec(g_fourier[l]), gc=vec(g_conv[l]),
                wo=w_out[l].astype(_BF16), g2=vec(g_ffn2[l]), w2g=w2_gate[l].astype(_BF16),
                w2u=w2_up[l].astype(_BF16), w2d=w2_down[l].astype(_BF16), gfin=vec(g_final))


def kernel(x_prompt, x_sample, g_ffn1, w1_gate, w1_up, w1_down, g_mix, w_in, conv_w, g_fourier,
           g_conv, w_out, g_ffn2, w2_gate, w2_up, w2_down, g_final):
    assert g_ffn1.shape[0] == 1, "single-layer trunk"
    p = _layer_params(0, g_ffn1, w1_gate, w1_up, w1_down, g_mix, w_in, conv_w, g_fourier, g_conv,
                      w_out, g_ffn2, w2_gate, w2_up, w2_down, g_final)
    run = functools.partial(_trunk, p=p, tm=1024, n_chunks=4)
    return (run(x_prompt), run(x_sample))
```
